```python
import math
import jax, jax.numpy as jnp
from jax import lax
import numpy as np

D_MODEL = 1024
BATCH = 16
SEQ = 2048
DEPTH = 1

MIX_WIDTH = D_MODEL
ATT_WIDTH = MIX_WIDTH // 2
HG_WIDTH = MIX_WIDTH - ATT_WIDTH
ATT_HEAD_DIM = 64
ATT_Q_HEADS = ATT_WIDTH // ATT_HEAD_DIM
ATT_KV_HEADS = 2
ATT_KV_COLS = ATT_KV_HEADS * ATT_HEAD_DIM
WINDOW = 128
ROPE_DIM = ATT_HEAD_DIM // 4
ROPE_THETA = 500000.0
HG_HEAD_DIM = 128
HG_HEADS = HG_WIDTH // HG_HEAD_DIM
HG_CHUNK = 32
IN_COLS = ATT_WIDTH + 2 * ATT_KV_COLS + 4 * HG_WIDTH
_SPLITS = list(np.cumsum([ATT_WIDTH, ATT_KV_COLS, ATT_KV_COLS, HG_WIDTH, HG_WIDTH, HG_WIDTH])[:].tolist())
D_FF = 4 * D_MODEL
N_MOD = 6
EPS = 1e-6

kernel_name = "hybrid_swa_sink_hgrn2_adaln_layer"


def rmsnorm(x, w):
    xf = x.astype(jnp.float32)
    y = xf * lax.rsqrt(jnp.mean(xf * xf, axis=-1, keepdims=True) + EPS)
    return (y * w.astype(jnp.float32)).astype(x.dtype)


def partial_rope(x):
    T = x.shape[1]
    half = ROPE_DIM // 2
    inv_freq = ROPE_THETA ** (-jnp.arange(0, ROPE_DIM, 2, dtype=jnp.float32) / ROPE_DIM)
    ang = jnp.arange(T, dtype=jnp.float32)[:, None] * inv_freq[None, :]
    cos = jnp.cos(ang)[None, :, None, :].astype(x.dtype)
    sin = jnp.sin(ang)[None, :, None, :].astype(x.dtype)
    x1, x2, rest = x[..., :half], x[..., half:ROPE_DIM], x[..., ROPE_DIM:]
    return jnp.concatenate([x1 * cos - x2 * sin, x2 * cos + x1 * sin, rest], axis=-1)


def sliding_window_sink_attention(q, k, v, sinks):
    B, T, Hq, D = q.shape
    nb = T // WINDOW
    G = Hq // ATT_KV_HEADS
    qb = q.reshape(B, nb, WINDOW, ATT_KV_HEADS, G, D)

    def band(a):
        ab = a.reshape(B, nb, WINDOW, ATT_KV_HEADS, D)
        prev = jnp.pad(ab, ((0, 0), (1, 0), (0, 0), (0, 0), (0, 0)))[:, :-1]
        return jnp.concatenate([prev, ab], axis=2)

    kk, vv = band(k), band(v)
    s = jnp.einsum('bnqhgd,bnkhd->bnhgqk', qb, kk).astype(jnp.float32) * (D ** -0.5)
    blk = jnp.arange(nb)[:, None]
    q_pos = blk * WINDOW + jnp.arange(WINDOW)[None, :]
    k_pos = (blk - 1) * WINDOW + jnp.arange(2 * WINDOW)[None, :]
    diff = q_pos[:, :, None] - k_pos[:, None, :]
    mask = (diff >= 0) & (diff < WINDOW) & (k_pos[:, None, :] >= 0)
    s = jnp.where(mask[None, :, None, None], s, jnp.finfo(jnp.float32).min)
    sink = sinks.astype(jnp.float32).reshape(ATT_KV_HEADS, G)[None, None, :, :, None, None]
    m = jnp.maximum(jnp.max(s, axis=-1, keepdims=True), sink)
    p = jnp.exp(s - m)
    p = p / (jnp.sum(p, axis=-1, keepdims=True) + jnp.exp(sink - m))
    o = jnp.einsum('bnhgqk,bnkhd->bnqhgd', p.astype(v.dtype), vv)
    return o.reshape(B, T, Hq * D)


def hgrn2_chunkwise(q, k, v, log_f):
    B, T, H, Dk = q.shape
    Dv = v.shape[-1]
    nc = T // HG_CHUNK

    def to_chunks(a):
        return a.astype(jnp.float32).reshape(B, nc, HG_CHUNK, H, a.shape[-1]).transpose(1, 0, 3, 2, 4)

    qc, kc, vc, gc = to_chunks(q), to_chunks(k), to_chunks(v), to_chunks(log_f)
    bc = jnp.cumsum(gc, axis=3)
    tri = jnp.tril(jnp.ones((HG_CHUNK, HG_CHUNK), dtype=bool))

    def step(S, inp):
        q_, k_, v_, b_ = inp
        b_last = b_[:, :, -1:, :]
        q_dec = q_ * jnp.exp(b_)
        k_dec = k_ * jnp.exp(-b_)
        a = jnp.where(tri, jnp.einsum('bhtk,bhsk->bhts', q_dec, k_dec), 0.0)
        o = jnp.einsum('bhts,bhsv->bhtv', a, v_) + jnp.einsum('bhtk,bhkv->bhtv', q_dec, S)
        S = S * jnp.exp(b_last[:, :, 0, :])[..., None] + \
            jnp.einsum('bhsk,bhsv->bhkv', k_ * jnp.exp(b_last - b_), v_)
        return S, o

    S0 = jnp.zeros((B, H, Dk, Dv), jnp.float32)
    _, o = lax.scan(step, S0, (qc, kc, vc, bc))
    return o.transpose(1, 0, 3, 2, 4).reshape(B, T, H, Dv)


def setup_inputs(seed: int = 0) -> dict:
    key = jax.random.key(seed)
    ks = jax.random.split(key, 17)
    f32 = jnp.float32

    def gain(k, shape):
        return (1.0 + 0.02 * jax.random.normal(k, shape)).astype(f32)

    return {
        "x": jax.random.normal(ks[0], (BATCH, SEQ, D_MODEL), f32),
        "c": jax.random.normal(ks[1], (BATCH, D_MODEL), f32),
        "w_ada": jax.random.normal(ks[2], (DEPTH, D_MODEL, N_MOD * D_MODEL), f32) * (0.5 * D_MODEL ** -0.5),
        "b_ada": jax.random.normal(ks[3], (DEPTH, N_MOD * D_MODEL), f32) * 0.02,
        "pre_w_mix": gain(ks[4], (DEPTH, D_MODEL)),
        "w_in": jax.random.normal(ks[5], (DEPTH, D_MODEL, IN_COLS), f32) * D_MODEL ** -0.5,
        "attn_sinks": jax.random.normal(ks[6], (DEPTH, ATT_Q_HEADS), f32) * 0.5,
        "attn_out_w": gain(ks[7], (DEPTH, ATT_WIDTH)),
        "lb_table": jax.random.normal(ks[8], (DEPTH + 1, HG_WIDTH), f32) * 0.1,
        "hg_norm_w": gain(ks[9], (DEPTH, HG_HEAD_DIM)),
        "w_out": jax.random.normal(ks[10], (DEPTH, MIX_WIDTH, D_MODEL), f32) * MIX_WIDTH ** -0.5,
        "post_w_mix": gain(ks[11], (DEPTH, D_MODEL)),
        "pre_w_mlp": gain(ks[12], (DEPTH, D_MODEL)),
        "w_up": jax.random.normal(ks[13], (DEPTH, D_MODEL, D_FF), f32) * D_MODEL ** -0.5,
        "w_down": jax.random.normal(ks[14], (DEPTH, D_FF, D_MODEL), f32) * D_FF ** -0.5,
        "post_w_mlp": gain(ks[15], (DEPTH, D_MODEL)),
    }


def reference(x, c, w_ada, b_ada, pre_w_mix, w_in, attn_sinks, attn_out_w, lb_table,
              hg_norm_w, w_out, post_w_mix, pre_w_mlp, w_up, w_down, post_w_mlp):
    B, T, _ = x.shape
    lb_p = jax.nn.softmax(lb_table.astype(jnp.float32), axis=0)
    lower_bounds = jnp.cumsum(lb_p, axis=0) - lb_p[0:1]
    c_act = jax.nn.silu(c)

    for l in range(DEPTH):
        mod = c_act @ w_ada[l] + b_ada[l]
        sh1, sc1, g1, sh2, sc2, g2 = [m[:, None, :] for m in jnp.split(mod, N_MOD, axis=-1)]

        h = rmsnorm(x, pre_w_mix[l]) * (1.0 + sc1) + sh1
        proj = h @ w_in[l]
        aq, ak, av, hq, hf, hi, hg = jnp.split(proj, _SPLITS, axis=-1)

        aq = partial_rope(aq.reshape(B, T, ATT_Q_HEADS, ATT_HEAD_DIM))
        ak = partial_rope(ak.reshape(B, T, ATT_KV_HEADS, ATT_HEAD_DIM))
        av = av.reshape(B, T, ATT_KV_HEADS, ATT_HEAD_DIM)
        attn = sliding_window_sink_attention(aq, ak, av, attn_sinks[l])
        attn = rmsnorm(attn, attn_out_w[l])

        lb = lower_bounds[l + 1].reshape(HG_HEADS, HG_HEAD_DIM)
        f = lb + (1.0 - lb) * jax.nn.sigmoid(hf.reshape(B, T, HG_HEADS, HG_HEAD_DIM).astype(jnp.float32))
        hq4 = jax.nn.silu(hq.reshape(B, T, HG_HEADS, HG_HEAD_DIM))
        hv4 = hi.reshape(B, T, HG_HEADS, HG_HEAD_DIM)
        rec = hgrn2_chunkwise(hq4, 1.0 - f, hv4, jnp.log(f)).astype(x.dtype)
        rec = rmsnorm(rec, hg_norm_w[l]) * jax.nn.silu(hg.reshape(B, T, HG_HEADS, HG_HEAD_DIM))
        rec = rec.reshape(B, T, HG_WIDTH)

        mix = jnp.concatenate([attn, rec], axis=-1) @ w_out[l]
        x = x + g1 * rmsnorm(mix, post_w_mix[l])

        h = rmsnorm(x, pre_w_mlp[l]) * (1.0 + sc2) + sh2
        u = jnp.square(jax.nn.relu(h @ w_up[l]))
        x = x + g2 * rmsnorm(u @ w_down[l], post_w_mlp[l])
    return x
```

```python
import functools

import jax
import jax.numpy as jnp
import numpy as np
from jax import lax
from jax.experimental import pallas as pl
from jax.experimental.pallas import tpu as pltpu

F32 = jnp.float32
BF16 = jnp.bfloat16

D_MODEL = 1024
ATT_WIDTH = 512
ATT_HEAD_DIM = 64
ATT_Q_HEADS = 8
ATT_KV_HEADS = 2
ATT_GROUP = ATT_Q_HEADS // ATT_KV_HEADS
ATT_KV_COLS = ATT_KV_HEADS * ATT_HEAD_DIM
WINDOW = 128
ROPE_DIM = 16
ROPE_THETA = 500000.0
HG_WIDTH = 512
HG_HEAD_DIM = 128
HG_HEADS = 4
HG_CHUNK = 32
IN_COLS = ATT_WIDTH + 2 * ATT_KV_COLS + 4 * HG_WIDTH
D_FF = 4 * D_MODEL
N_MOD = 6
EPS = 1e-6

LANES = 128
MIX_TIME_BLOCK = 256
MLP_TOKEN_BLOCK = 512
VMEM_LIMIT_BYTES = 56 * 1024 * 1024

_Q0, _K0, _V0 = 0, ATT_WIDTH, ATT_WIDTH + ATT_KV_COLS
_HQ0 = ATT_WIDTH + 2 * ATT_KV_COLS
_HF0, _HI0, _HG0 = _HQ0 + HG_WIDTH, _HQ0 + 2 * HG_WIDTH, _HQ0 + 3 * HG_WIDTH


def _dot(a, b):
    return jnp.dot(a, b, preferred_element_type=F32)


def _dot_nt(a, b):
    return lax.dot_general(a, b, (((1,), (1,)), ((), ())), preferred_element_type=F32)


def _dot_tn(a, b):
    return lax.dot_general(a, b, (((0,), (0,)), ((), ())), preferred_element_type=F32)


def _sigmoid(x):
    return 1.0 / (1.0 + jnp.exp(-x))


def _rms(x):
    return x * lax.rsqrt(jnp.mean(x * x, axis=-1, keepdims=True) + EPS)


def _adaln_kernel(c_ref, w_ref, b_ref, o_ref):
    c = c_ref[...]
    c_act = c * _sigmoid(c)
    o_ref[...] = jnp.dot(c_act, w_ref[...], preferred_element_type=F32,
                         precision=lax.Precision.HIGHEST) + b_ref[...]


def _adaln(c, w_ada, b_ada):
    B, D = c.shape
    n = w_ada.shape[1]
    bn = D
    return pl.pallas_call(
        _adaln_kernel,
        grid=(n // bn,),
        in_specs=[pl.BlockSpec((B, D), lambda j: (0, 0)),
                  pl.BlockSpec((D, bn), lambda j: (0, j)),
                  pl.BlockSpec((1, bn), lambda j: (0, j))],
        out_specs=pl.BlockSpec((B, bn), lambda j: (0, j)),
        out_shape=jax.ShapeDtypeStruct((B, n), F32),
        name="adaln_mod",
    )(c, w_ada, b_ada.reshape(1, n))


def _rope(xg, cos, sin_a, sin_b):
    return (xg * cos + pltpu.roll(xg, LANES - ROPE_DIM // 2, 1) * sin_a
            + pltpu.roll(xg, ROPE_DIM // 2, 1) * sin_b)


def _mixer_kernel(sinks_ref, x_ref, mod_ref, prew_ref, win_ref, rope_ref, aow_ref,
                  lbt_ref, hgw_ref, wout_ref, postw_ref, o_ref,
                  kprev_ref, vprev_ref, st_ref):
    t = pl.program_id(1)
    tb = x_ref.shape[0]
    n_att_blocks = tb // WINDOW
    n_chunks = tb // HG_CHUNK

    @pl.when(t == 0)
    def _():
        kprev_ref[...] = jnp.zeros_like(kprev_ref)
        vprev_ref[...] = jnp.zeros_like(vprev_ref)
        st_ref[...] = jnp.zeros_like(st_ref)

    x = x_ref[...]
    shift1, scale1, gate1 = mod_ref[0:1, :], mod_ref[1:2, :], mod_ref[2:3, :]
    h = _rms(x) * prew_ref[...] * (1.0 + scale1) + shift1
    proj = _dot(h.astype(BF16), win_ref[...])

    cos_q, sa_q, sb_q = (rope_ref[:, 0:LANES], rope_ref[:, LANES:2 * LANES],
                         rope_ref[:, 2 * LANES:3 * LANES])
    cos_k, sa_k, sb_k = (rope_ref[:, 3 * LANES:4 * LANES], rope_ref[:, 4 * LANES:5 * LANES],
                         rope_ref[:, 5 * LANES:6 * LANES])
    q_groups = [_rope(proj[:, _Q0 + j * LANES:_Q0 + (j + 1) * LANES], cos_q, sa_q, sb_q)
                for j in range(ATT_GROUP)]
    k_all = _rope(proj[:, _K0:_K0 + LANES], cos_k, sa_k, sb_k).astype(BF16)
    v_all = proj[:, _V0:_V0 + LANES].astype(BF16)

    lane = lax.broadcasted_iota(jnp.int32, (WINDOW, LANES), 1)
    low_half = lane < ATT_HEAD_DIM
    qi = lax.broadcasted_iota(jnp.int32, (WINDOW, 2 * WINDOW), 0)
    kj = lax.broadcasted_iota(jnp.int32, (WINDOW, 2 * WINDOW), 1)
    key_pos = kj - WINDOW
    dist = qi - key_pos
    band = (dist >= 0) & (dist < WINDOW)
    first_key = jnp.where(t > 0, -WINDOW, 0)
    neg = jnp.finfo(F32).min

    attn_blocks = []
    for n in range(n_att_blocks):
        r0 = n * WINDOW
        if n == 0:
            k_prev, v_prev = kprev_ref[...], vprev_ref[...]
            mask = band & (key_pos >= first_key)
        else:
            k_prev, v_prev = k_all[r0 - WINDOW:r0], v_all[r0 - WINDOW:r0]
            mask = band
        k_cat = jnp.concatenate([k_prev, k_all[r0:r0 + WINDOW]], axis=0)
        v_cat = jnp.concatenate([v_prev, v_all[r0:r0 + WINDOW]], axis=0)
        lhs = []
        for j in range(ATT_GROUP):
            qb = q_groups[j][r0:r0 + WINDOW]
            for g in range(ATT_KV_HEADS):
                keep = low_half if g == 0 else jnp.logical_not(low_half)
                lhs.append(jnp.where(keep, qb, 0.0).astype(BF16))
        s_all = _dot_nt(jnp.concatenate(lhs, axis=0), k_cat)
        p_list, inv_list = [], []
        for j in range(ATT_GROUP):
            for g in range(ATT_KV_HEADS):
                hrow = (j * ATT_KV_HEADS + g) * WINDOW
                s = jnp.where(mask, s_all[hrow:hrow + WINDOW], neg)
                sink = sinks_ref[0, g * ATT_GROUP + j]
                m = jnp.maximum(jnp.max(s, axis=-1, keepdims=True), sink)
                p = jnp.exp(s - m)
                denom = jnp.sum(p, axis=-1, keepdims=True) + jnp.exp(sink - m)
                p_list.append(p.astype(BF16))
                inv_list.append(1.0 / denom)
        o_all = _dot(jnp.concatenate(p_list, axis=0), v_cat)
        cols = []
        for j in range(ATT_GROUP):
            i0 = (j * ATT_KV_HEADS) * WINDOW
            i1 = i0 + WINDOW
            o0 = o_all[i0:i0 + WINDOW] * inv_list[j * ATT_KV_HEADS]
            o1 = o_all[i1:i1 + WINDOW] * inv_list[j * ATT_KV_HEADS + 1]
            cols.append(jnp.where(low_half, o0, o1))
        attn_blocks.append(jnp.concatenate(cols, axis=1))
    kprev_ref[...] = k_all[tb - WINDOW:tb]
    vprev_ref[...] = v_all[tb - WINDOW:tb]
    attn = jnp.concatenate(attn_blocks, axis=0)
    attn = _rms(attn) * aow_ref[...]

    lbt = lbt_ref[...]
    lb_m = jnp.max(lbt, axis=0, keepdims=True)
    lb_e = jnp.exp(lbt - lb_m)
    lb_p = lb_e / jnp.sum(lb_e, axis=0, keepdims=True)
    lb = (lb_p[0:1] + lb_p[1:2]) - lb_p[0:1]
    f = lb + (1.0 - lb) * _sigmoid(proj[:, _HF0:_HF0 + HG_WIDTH])
    hq = proj[:, _HQ0:_HQ0 + HG_WIDTH]
    q_h = hq * _sigmoid(hq)
    k_h = 1.0 - f
    log_f = jnp.log(f)
    ri = lax.broadcasted_iota(jnp.int32, (tb, tb), 0)
    ci = lax.broadcasted_iota(jnp.int32, (tb, tb), 1)
    tri = jnp.where((ri // HG_CHUNK == ci // HG_CHUNK) & (ci <= ri), 1.0, 0.0).astype(BF16)
    lf_hi = log_f.astype(BF16)
    lf_lo = (log_f - lf_hi.astype(F32)).astype(BF16)
    b = _dot(tri, lf_hi) + _dot(tri, lf_lo)
    b_last = jnp.broadcast_to(
        b.reshape(n_chunks, HG_CHUNK, HG_WIDTH)[:, HG_CHUNK - 1:HG_CHUNK, :],
        (n_chunks, HG_CHUNK, HG_WIDTH)).reshape(tb, HG_WIDTH)
    q_dec = (q_h * jnp.exp(b)).astype(BF16)
    k_dec = (k_h * jnp.exp(-b)).astype(BF16)
    k_upd = (k_h * jnp.exp(b_last - b)).astype(BF16)
    decay = jnp.exp(b_last)
    v_h = proj[:, _HI0:_HI0 + HG_WIDTH].astype(BF16)

    ar = lax.broadcasted_iota(jnp.int32, (WINDOW, WINDOW), 0)
    ac = lax.broadcasted_iota(jnp.int32, (WINDOW, WINDOW), 1)
    intra_mask = (ar // HG_CHUNK == ac // HG_CHUNK) & (ac <= ar)
    chunks_per_sb = WINDOW // HG_CHUNK

    rec_cols = []
    for hh in range(HG_HEADS):
        c0 = hh * HG_HEAD_DIM
        st = st_ref[hh]
        o_rows = []
        for sb in range(tb // WINDOW):
            r0 = sb * WINDOW
            qd = q_dec[r0:r0 + WINDOW, c0:c0 + HG_HEAD_DIM]
            kd = k_dec[r0:r0 + WINDOW, c0:c0 + HG_HEAD_DIM]
            ku = k_upd[r0:r0 + WINDOW, c0:c0 + HG_HEAD_DIM]
            vv = v_h[r0:r0 + WINDOW, c0:c0 + HG_HEAD_DIM]
            a = jnp.where(intra_mask, _dot_nt(qd, kd), 0.0).astype(BF16)
            o_intra = _dot(a, vv)
            inter = []
            for cc in range(chunks_per_sb):
                s0 = cc * HG_CHUNK
                inter.append(_dot_nt(qd[s0:s0 + HG_CHUNK], st.astype(BF16)))
                d_row = decay[r0 + s0:r0 + s0 + 1, c0:c0 + HG_HEAD_DIM]
                st = st * d_row + _dot_tn(vv[s0:s0 + HG_CHUNK], ku[s0:s0 + HG_CHUNK])
            o_rows.append(o_intra + jnp.concatenate(inter, axis=0))
        st_ref[hh] = st
        o_h = jnp.concatenate(o_rows, axis=0)
        rec_cols.append(_rms(o_h) * hgw_ref[...])
    hg = proj[:, _HG0:_HG0 + HG_WIDTH]
    rec = jnp.concatenate(rec_cols, axis=1) * (hg * _sigmoid(hg))

    mix_in = jnp.concatenate([attn, rec], axis=1).astype(BF16)
    mix = _dot(mix_in, wout_ref[...])
    o_ref[...] = x + gate1 * (_rms(mix) * postw_ref[...])


def _token_mixer(x, mod, pre_w, w_in_bf, rope_tab, sinks, aow, lb_table, hg_w, w_out_bf, post_w):
    B, T, D = x.shape
    tb = MIX_TIME_BLOCK
    const2 = lambda b, t: (0, 0)
    return pl.pallas_call(
        _mixer_kernel,
        grid=(B, T // tb),
        in_specs=[
            pl.BlockSpec(memory_space=pltpu.SMEM),
            pl.BlockSpec((None, tb, D), lambda b, t: (b, t, 0)),
            pl.BlockSpec((None, N_MOD, D), lambda b, t: (b, 0, 0)),
            pl.BlockSpec((1, D), const2),
            pl.BlockSpec((D, IN_COLS), const2),
            pl.BlockSpec((tb, 6 * LANES), lambda b, t: (t, 0)),
            pl.BlockSpec((1, ATT_WIDTH), const2),
            pl.BlockSpec((2, HG_WIDTH), const2),
            pl.BlockSpec((1, HG_HEAD_DIM), const2),
            pl.BlockSpec((D, D), const2),
            pl.BlockSpec((1, D), const2),
        ],
        out_specs=pl.BlockSpec((None, tb, D), lambda b, t: (b, t, 0)),
        out_shape=jax.ShapeDtypeStruct((B, T, D), F32),
        scratch_shapes=[
            pltpu.VMEM((WINDOW, LANES), BF16),
            pltpu.VMEM((WINDOW, LANES), BF16),
            pltpu.VMEM((HG_HEADS, HG_HEAD_DIM, HG_HEAD_DIM), F32),
        ],
        compiler_params=pltpu.CompilerParams(
            dimension_semantics=("arbitrary", "arbitrary"),
            vmem_limit_bytes=VMEM_LIMIT_BYTES),
        name="token_mixer",
    )(sinks, x, mod, pre_w, w_in_bf, rope_tab, aow, lb_table, hg_w, w_out_bf, post_w)


def _mlp_kernel(x_ref, mod_ref, prew_ref, wup_ref, wdown_ref, postw_ref, o_ref):
    x = x_ref[...]
    shift2, scale2, gate2 = mod_ref[3:4, :], mod_ref[4:5, :], mod_ref[5:6, :]
    h = _rms(x) * prew_ref[...] * (1.0 + scale2) + shift2
    u = _dot(h.astype(BF16), wup_ref[...])
    u = jnp.square(jnp.maximum(u, 0.0)).astype(BF16)
    y = _dot(u, wdown_ref[...])
    o_ref[...] = x + gate2 * (_rms(y) * postw_ref[...])


def _channel_mlp(x, mod, pre_w, w_up_bf, w_down_bf, post_w):
    B, T, D = x.shape
    tm = MLP_TOKEN_BLOCK
    const2 = lambda b, t: (0, 0)
    return pl.pallas_call(
        _mlp_kernel,
        grid=(B, T // tm),
        in_specs=[
            pl.BlockSpec((None, tm, D), lambda b, t: (b, t, 0)),
            pl.BlockSpec((None, N_MOD, D), lambda b, t: (b, 0, 0)),
            pl.BlockSpec((1, D), const2),
            pl.BlockSpec((D, D_FF), const2),
            pl.BlockSpec((D_FF, D), const2),
            pl.BlockSpec((1, D), const2),
        ],
        out_specs=pl.BlockSpec((None, tm, D), lambda b, t: (b, t, 0)),
        out_shape=jax.ShapeDtypeStruct((B, T, D), F32),
        compiler_params=pltpu.CompilerParams(
            dimension_semantics=("arbitrary", "arbitrary"),
            vmem_limit_bytes=VMEM_LIMIT_BYTES),
        name="channel_mlp",
    )(x, mod, pre_w, w_up_bf, w_down_bf, post_w)


def _q_perm():
    perm = np.empty((ATT_WIDTH,), np.int32)
    for j in range(ATT_GROUP):
        for g in range(ATT_KV_HEADS):
            for d in range(ATT_HEAD_DIM):
                perm[j * LANES + g * ATT_HEAD_DIM + d] = (g * ATT_GROUP + j) * ATT_HEAD_DIM + d
    return perm


def _rope_tables(T):
    half = ROPE_DIM // 2
    inv_freq = ROPE_THETA ** (-jnp.arange(0, ROPE_DIM, 2, dtype=F32) / ROPE_DIM)
    ang = jnp.arange(T, dtype=F32)[:, None] * inv_freq[None, :]
    cos, sin = jnp.cos(ang), jnp.sin(ang)
    ones = jnp.ones((T, ATT_HEAD_DIM - ROPE_DIM), F32)
    zeros_h = jnp.zeros((T, half), F32)
    zeros_r = jnp.zeros((T, ATT_HEAD_DIM - ROPE_DIM), F32)
    cos_head = jnp.concatenate([cos, cos, ones], axis=1)
    sa_head = jnp.concatenate([-sin, zeros_h, zeros_r], axis=1)
    sb_head = jnp.concatenate([zeros_h, sin, zeros_r], axis=1)
    k_tabs = [jnp.tile(a, (1, LANES // ATT_HEAD_DIM)) for a in (cos_head, sa_head, sb_head)]
    q_tabs = [a * (ATT_HEAD_DIM ** -0.5) for a in k_tabs]
    return jnp.concatenate(q_tabs + k_tabs, axis=1)


def kernel(x, c, w_ada, b_ada, pre_w_mix, w_in, attn_sinks, attn_out_w, lb_table, hg_norm_w,
           w_out, post_w_mix, pre_w_mlp, w_up, w_down, post_w_mlp):
    B, T, D = x.shape
    assert (D, T % MIX_TIME_BLOCK, T % MLP_TOKEN_BLOCK) == (D_MODEL, 0, 0)
    assert w_ada.shape[0] == 1, "single layer"
    perm = _q_perm()

    mod = _adaln(c, w_ada[0], b_ada[0]).reshape(B, N_MOD, D)

    w_in_bf = jnp.concatenate([w_in[0][:, perm], w_in[0][:, ATT_WIDTH:]], axis=1).astype(BF16)
    w_out_bf = jnp.concatenate([w_out[0][perm, :], w_out[0][ATT_WIDTH:, :]], axis=0).astype(BF16)
    aow = attn_out_w[0][perm].reshape(1, ATT_WIDTH)
    x1 = _token_mixer(
        x, mod, pre_w_mix[0].reshape(1, D), w_in_bf, _rope_tables(T),
        attn_sinks[0].reshape(1, ATT_Q_HEADS), aow, lb_table,
        hg_norm_w[0].reshape(1, HG_HEAD_DIM), w_out_bf, post_w_mix[0].reshape(1, D))

    return _channel_mlp(x1, mod, pre_w_mlp[0].reshape(1, D), w_up[0].astype(BF16),
                        w_down[0].astype(BF16), post_w_mlp[0].reshape(1, D))
```

```python
import jax
import jax.numpy as jnp
import numpy as np
from jax import lax
from jax.experimental import pallas as pl
from jax.experimental.pallas import tpu as pltpu

F32 = jnp.float32
BF16 = jnp.bfloat16

D_MODEL = 1024
ATT_WIDTH = 512
ATT_HEAD_DIM = 64
ATT_Q_HEADS = 8
ATT_KV_HEADS = 2
ATT_GROUP = ATT_Q_HEADS // ATT_KV_HEADS
ATT_KV_COLS = ATT_KV_HEADS * ATT_HEAD_DIM
WINDOW = 128
ROPE_DIM = 16
ROPE_THETA = 500000.0
HG_WIDTH = 512
HG_HEAD_DIM = 128
HG_HEADS = 4
HG_SUB = 64
IN_COLS = ATT_WIDTH + 2 * ATT_KV_COLS + 4 * HG_WIDTH
D_FF = 4 * D_MODEL
N_MOD = 6
EPS = 1e-6

LANES = 128
MIX_TIME_BLOCK = 256
MLP_TOKEN_BLOCK = 512
VMEM_LIMIT_BYTES = 56 * 1024 * 1024

_Q0, _K0, _V0 = 0, ATT_WIDTH, ATT_WIDTH + ATT_KV_COLS
_HQ0 = ATT_WIDTH + 2 * ATT_KV_COLS
_HF0, _HI0, _HG0 = _HQ0 + HG_WIDTH, _HQ0 + 2 * HG_WIDTH, _HQ0 + 3 * HG_WIDTH


def _dot(a, b):
    return jnp.dot(a, b, preferred_element_type=F32)


def _dot_nt(a, b):
    return lax.dot_general(a, b, (((1,), (1,)), ((), ())), preferred_element_type=F32)


def _dot_tn(a, b):
    return lax.dot_general(a, b, (((0,), (0,)), ((), ())), preferred_element_type=F32)


def _sigmoid(x):
    return 1.0 / (1.0 + jnp.exp(-x))


def _rms(x):
    return x * lax.rsqrt(jnp.mean(x * x, axis=-1, keepdims=True) + EPS)


def _adaln_kernel(c_ref, w_ref, b_ref, o_ref):
    c = c_ref[...]
    c_act = c * _sigmoid(c)
    o_ref[...] = jnp.dot(c_act, w_ref[...], preferred_element_type=F32,
                         precision=lax.Precision.HIGHEST) + b_ref[...]


def _adaln(c, w_ada, b_ada):
    B, D = c.shape
    n = w_ada.shape[1]
    bn = D
    return pl.pallas_call(
        _adaln_kernel,
        grid=(n // bn,),
        in_specs=[pl.BlockSpec((B, D), lambda j: (0, 0)),
                  pl.BlockSpec((D, bn), lambda j: (0, j)),
                  pl.BlockSpec((1, bn), lambda j: (0, j))],
        out_specs=pl.BlockSpec((B, bn), lambda j: (0, j)),
        out_shape=jax.ShapeDtypeStruct((B, n), F32),
        name="adaln_mod",
    )(c, w_ada, b_ada.reshape(1, n))


def _rope(xg, cos, sin_a, sin_b):
    return (xg * cos + pltpu.roll(xg, LANES - ROPE_DIM // 2, 1) * sin_a
            + pltpu.roll(xg, ROPE_DIM // 2, 1) * sin_b)


def _mixer_kernel(sinks_ref, x_ref, mod_ref, prew_ref, win_ref, rope_ref, aow_ref,
                  lbt_ref, hgw_ref, tril_ref, wout_ref, postw_ref, o_ref,
                  kprev_ref, vprev_ref, st_ref):
    t = pl.program_id(1)
    tb = x_ref.shape[0]
    n_att_blocks = tb // WINDOW

    @pl.when(t == 0)
    def _():
        kprev_ref[...] = jnp.zeros_like(kprev_ref)
        vprev_ref[...] = jnp.zeros_like(vprev_ref)
        st_ref[...] = jnp.zeros_like(st_ref)

    x = x_ref[...]
    shift1, scale1, gate1 = mod_ref[0:1, :], mod_ref[1:2, :], mod_ref[2:3, :]
    h = _rms(x) * prew_ref[...] * (1.0 + scale1) + shift1
    proj = _dot(h.astype(BF16), win_ref[...])

    cos_q, sa_q, sb_q = (rope_ref[:, 0:LANES], rope_ref[:, LANES:2 * LANES],
                         rope_ref[:, 2 * LANES:3 * LANES])
    cos_k, sa_k, sb_k = (rope_ref[:, 3 * LANES:4 * LANES], rope_ref[:, 4 * LANES:5 * LANES],
                         rope_ref[:, 5 * LANES:6 * LANES])
    q_groups = [_rope(proj[:, _Q0 + j * LANES:_Q0 + (j + 1) * LANES], cos_q, sa_q, sb_q)
                for j in range(ATT_GROUP)]
    k_all = _rope(proj[:, _K0:_K0 + LANES], cos_k, sa_k, sb_k).astype(BF16)
    v_all = proj[:, _V0:_V0 + LANES].astype(BF16)

    lane = lax.broadcasted_iota(jnp.int32, (WINDOW, LANES), 1)
    low_half = lane < ATT_HEAD_DIM
    qi = lax.broadcasted_iota(jnp.int32, (WINDOW, WINDOW), 0)
    kj = lax.broadcasted_iota(jnp.int32, (WINDOW, WINDOW), 1)
    from_cur = kj <= qi
    cur_f = jnp.where(from_cur, 1.0, 0.0)
    has_prev = jnp.full((WINDOW, WINDOW), t, jnp.int32) > 0
    neg = jnp.finfo(F32).min

    attn_blocks = []
    for n in range(n_att_blocks):
        r0 = n * WINDOW
        if n == 0:
            k_prev, v_prev = kprev_ref[...], vprev_ref[...]
        else:
            k_prev, v_prev = k_all[r0 - WINDOW:r0], v_all[r0 - WINDOW:r0]
        k_cat = jnp.concatenate([k_prev, k_all[r0:r0 + WINDOW]], axis=0)
        v_cat = jnp.concatenate([v_prev, v_all[r0:r0 + WINDOW]], axis=0)
        lhs = []
        for j in range(ATT_GROUP):
            qb = q_groups[j][r0:r0 + WINDOW]
            for g in range(ATT_KV_HEADS):
                keep = low_half if g == 0 else jnp.logical_not(low_half)
                lhs.append(jnp.where(keep, qb, 0.0).astype(BF16))
        s_all = _dot_nt(jnp.concatenate(lhs, axis=0), k_cat)
        p_rows, denoms = [], []
        for j in range(ATT_GROUP):
            for g in range(ATT_KV_HEADS):
                hrow = (j * ATT_KV_HEADS + g) * WINDOW
                s_prev = s_all[hrow:hrow + WINDOW, 0:WINDOW]
                if n == 0:
                    s_prev = jnp.where(has_prev, s_prev, neg)
                s = jnp.where(from_cur, s_all[hrow:hrow + WINDOW, WINDOW:2 * WINDOW], s_prev)
                sink = sinks_ref[0, g * ATT_GROUP + j]
                m = jnp.maximum(jnp.max(s, axis=-1, keepdims=True), sink)
                p = jnp.exp(s - m)
                p_cur = p * cur_f
                p_rows.append(jnp.concatenate(
                    [(p - p_cur).astype(BF16), p_cur.astype(BF16)], axis=1))
                denoms.append(jnp.sum(p, axis=-1, keepdims=True) + jnp.exp(sink - m))
        o_all = _dot(jnp.concatenate(p_rows, axis=0), v_cat)
        cols = []
        for j in range(ATT_GROUP):
            halves = []
            for g in range(ATT_KV_HEADS):
                hi = j * ATT_KV_HEADS + g
                halves.append(o_all[hi * WINDOW:(hi + 1) * WINDOW] / denoms[hi])
            cols.append(jnp.where(low_half, halves[0], halves[1]))
        attn_blocks.append(jnp.concatenate(cols, axis=1))
    kprev_ref[...] = k_all[tb - WINDOW:tb]
    vprev_ref[...] = v_all[tb - WINDOW:tb]
    attn = jnp.concatenate(attn_blocks, axis=0)
    attn = _rms(attn) * aow_ref[...]

    lbt = lbt_ref[...]
    lb_m = jnp.max(lbt, axis=0, keepdims=True)
    lb_e = jnp.exp(lbt - lb_m)
    lb_p = lb_e / jnp.sum(lb_e, axis=0, keepdims=True)
    lb = (lb_p[0:1] + lb_p[1:2]) - lb_p[0:1]
    f = lb + (1.0 - lb) * _sigmoid(proj[:, _HF0:_HF0 + HG_WIDTH])
    hq = proj[:, _HQ0:_HQ0 + HG_WIDTH]
    q_h = hq * _sigmoid(hq)
    k_h = 1.0 - f
    log_f = jnp.log(f)
    lf_hi = log_f.astype(BF16)
    lf_lo = (log_f - lf_hi.astype(F32)).astype(BF16)
    tril = tril_ref[...]
    bc = _dot(tril, lf_hi) + _dot(tril, lf_lo)

    q_mid, k_mid, q_beg, k_end, decay = [], [], [], [], []
    for c in range(tb // HG_SUB):
        r0 = c * HG_SUB
        bcc = bc[r0:r0 + HG_SUB]
        b_mid = bc[r0 + HG_SUB // 2 - 1:r0 + HG_SUB // 2]
        b_end = bc[r0 + HG_SUB - 1:r0 + HG_SUB]
        qc, kc = q_h[r0:r0 + HG_SUB], k_h[r0:r0 + HG_SUB]
        q_mid.append(qc * jnp.exp(bcc - b_mid))
        k_mid.append(kc * jnp.exp(b_mid - bcc))
        q_beg.append(qc * jnp.exp(bcc))
        k_end.append(kc * jnp.exp(b_end - bcc))
        decay.append(jnp.exp(b_end))
    v_h = proj[:, _HI0:_HI0 + HG_WIDTH].astype(BF16)

    ar = lax.broadcasted_iota(jnp.int32, (WINDOW, WINDOW), 0)
    ac = lax.broadcasted_iota(jnp.int32, (WINDOW, WINDOW), 1)
    intra_mask = (ar // HG_SUB == ac // HG_SUB) & (ac <= ar)
    zeros_sub = jnp.zeros((HG_SUB, HG_WIDTH), BF16)

    o_blocks = []
    for sb in range(tb // WINDOW):
        c0, c1 = 2 * sb, 2 * sb + 1
        r0 = sb * WINDOW
        qa = jnp.concatenate([q_mid[c0], q_mid[c1]], axis=0).astype(BF16)
        ka = jnp.concatenate([k_mid[c0], k_mid[c1]], axis=0).astype(BF16)
        qx = jnp.concatenate([zeros_sub, q_beg[c1].astype(BF16)], axis=0)
        kx = jnp.concatenate([k_end[c0].astype(BF16), zeros_sub], axis=0)
        q0 = jnp.concatenate([q_beg[c0], q_beg[c1] * decay[c0]], axis=0).astype(BF16)
        ku = jnp.concatenate([k_end[c0] * decay[c1], k_end[c1]], axis=0).astype(BF16)
        d_blk = decay[c0] * decay[c1]
        heads = []
        for hh in range(HG_HEADS):
            hl = slice(hh * HG_HEAD_DIM, (hh + 1) * HG_HEAD_DIM)
            vv = v_h[r0:r0 + WINDOW, hl]
            st = st_ref[hh]
            a = (jnp.where(intra_mask, _dot_nt(qa[:, hl], ka[:, hl]), 0.0)
                 + _dot_nt(qx[:, hl], kx[:, hl]))
            heads.append(_dot(a.astype(BF16), vv) + _dot_nt(q0[:, hl], st.astype(BF16)))
            st_ref[hh] = st * d_blk[:, hl] + _dot_tn(vv, ku[:, hl])
        o_blocks.append(jnp.concatenate(heads, axis=1))
    o_h = jnp.concatenate(o_blocks, axis=0)
    rec_cols = [_rms(o_h[:, hh * HG_HEAD_DIM:(hh + 1) * HG_HEAD_DIM]) * hgw_ref[...]
                for hh in range(HG_HEADS)]
    hg = proj[:, _HG0:_HG0 + HG_WIDTH]
    rec = jnp.concatenate(rec_cols, axis=1) * (hg * _sigmoid(hg))

    mix_in = jnp.concatenate([attn, rec], axis=1).astype(BF16)
    mix = _dot(mix_in, wout_ref[...])
    o_ref[...] = x + gate1 * (_rms(mix) * postw_ref[...])


def _token_mixer(x, mod, pre_w, w_in_bf, rope_tab, sinks, aow, lb_table, hg_w, w_out_bf, post_w):
    B, T, D = x.shape
    tb = MIX_TIME_BLOCK
    const2 = lambda b, t: (0, 0)
    idx = np.arange(tb)
    tril = ((idx[:, None] // HG_SUB == idx[None, :] // HG_SUB)
            & (idx[None, :] <= idx[:, None])).astype(np.float32)
    return pl.pallas_call(
        _mixer_kernel,
        grid=(B, T // tb),
        in_specs=[
            pl.BlockSpec(memory_space=pltpu.SMEM),
            pl.BlockSpec((None, tb, D), lambda b, t: (b, t, 0)),
            pl.BlockSpec((None, N_MOD, D), lambda b, t: (b, 0, 0)),
            pl.BlockSpec((1, D), const2),
            pl.BlockSpec((D, IN_COLS), const2),
            pl.BlockSpec((tb, 6 * LANES), lambda b, t: (t, 0)),
            pl.BlockSpec((1, ATT_WIDTH), const2),
            pl.BlockSpec((2, HG_WIDTH), const2),
            pl.BlockSpec((1, HG_HEAD_DIM), const2),
            pl.BlockSpec((tb, tb), const2),
            pl.BlockSpec((D, D), const2),
            pl.BlockSpec((1, D), const2),
        ],
        out_specs=pl.BlockSpec((None, tb, D), lambda b, t: (b, t, 0)),
        out_shape=jax.ShapeDtypeStruct((B, T, D), F32),
        scratch_shapes=[
            pltpu.VMEM((WINDOW, LANES), BF16),
            pltpu.VMEM((WINDOW, LANES), BF16),
            pltpu.VMEM((HG_HEADS, HG_HEAD_DIM, HG_HEAD_DIM), F32),
        ],
        compiler_params=pltpu.CompilerParams(
            dimension_semantics=("arbitrary", "arbitrary"),
            vmem_limit_bytes=VMEM_LIMIT_BYTES),
        name="token_mixer",
    )(sinks, x, mod, pre_w, w_in_bf, rope_tab, aow, lb_table, hg_w,
      jnp.asarray(tril, BF16), w_out_bf, post_w)


def _mlp_kernel(x_ref, mod_ref, prew_ref, wup_ref, wdown_ref, postw_ref, o_ref):
    x = x_ref[...]
    shift2, scale2, gate2 = mod_ref[3:4, :], mod_ref[4:5, :], mod_ref[5:6, :]
    h = _rms(x) * prew_ref[...] * (1.0 + scale2) + shift2
    u = _dot(h.astype(BF16), wup_ref[...])
    u = jnp.square(jnp.maximum(u, 0.0)).astype(BF16)
    y = _dot(u, wdown_ref[...])
    o_ref[...] = x + gate2 * (_rms(y) * postw_ref[...])


def _channel_mlp(x, mod, pre_w, w_up_bf, w_down_bf, post_w):
    B, T, D = x.shape
    tm = MLP_TOKEN_BLOCK
    const2 = lambda b, t: (0, 0)
    return pl.pallas_call(
        _mlp_kernel,
        grid=(B, T // tm),
        in_specs=[
            pl.BlockSpec((None, tm, D), lambda b, t: (b, t, 0)),
            pl.BlockSpec((None, N_MOD, D), lambda b, t: (b, 0, 0)),
            pl.BlockSpec((1, D), const2),
            pl.BlockSpec((D, D_FF), const2),
            pl.BlockSpec((D_FF, D), const2),
            pl.BlockSpec((1, D), const2),
        ],
        out_specs=pl.BlockSpec((None, tm, D), lambda b, t: (b, t, 0)),
        out_shape=jax.ShapeDtypeStruct((B, T, D), F32),
        compiler_params=pltpu.CompilerParams(
            dimension_semantics=("arbitrary", "arbitrary"),
            vmem_limit_bytes=VMEM_LIMIT_BYTES),
        name="channel_mlp",
    )(x, mod, pre_w, w_up_bf, w_down_bf, post_w)


def _q_perm():
    perm = np.empty((ATT_WIDTH,), np.int32)
    for j in range(ATT_GROUP):
        for g in range(ATT_KV_HEADS):
            for d in range(ATT_HEAD_DIM):
                perm[j * LANES + g * ATT_HEAD_DIM + d] = (g * ATT_GROUP + j) * ATT_HEAD_DIM + d
    return perm


def _rope_tables(T):
    half = ROPE_DIM // 2
    inv_freq = ROPE_THETA ** (-jnp.arange(0, ROPE_DIM, 2, dtype=F32) / ROPE_DIM)
    ang = jnp.arange(T, dtype=F32)[:, None] * inv_freq[None, :]
    cos, sin = jnp.cos(ang), jnp.sin(ang)
    ones = jnp.ones((T, ATT_HEAD_DIM - ROPE_DIM), F32)
    zeros_h = jnp.zeros((T, half), F32)
    zeros_r = jnp.zeros((T, ATT_HEAD_DIM - ROPE_DIM), F32)
    cos_head = jnp.concatenate([cos, cos, ones], axis=1)
    sa_head = jnp.concatenate([-sin, zeros_h, zeros_r], axis=1)
    sb_head = jnp.concatenate([zeros_h, sin, zeros_r], axis=1)
    k_tabs = [jnp.tile(a, (1, LANES // ATT_HEAD_DIM)) for a in (cos_head, sa_head, sb_head)]
    q_tabs = [a * (ATT_HEAD_DIM ** -0.5) for a in k_tabs]
    return jnp.concatenate(q_tabs + k_tabs, axis=1)


def kernel(x, c, w_ada, b_ada, pre_w_mix, w_in, attn_sinks, attn_out_w, lb_table, hg_norm_w,
           w_out, post_w_mix, pre_w_mlp, w_up, w_down, post_w_mlp):
    B, T, D = x.shape
    assert (D, T % MIX_TIME_BLOCK, T % MLP_TOKEN_BLOCK) == (D_MODEL, 0, 0)
    assert w_ada.shape[0] == 1, "single layer"
    perm = _q_perm()

    mod = _adaln(c, w_ada[0], b_ada[0]).reshape(B, N_MOD, D)

    w_in_bf = jnp.concatenate([w_in[0][:, perm], w_in[0][:, ATT_WIDTH:]], axis=1).astype(BF16)
    w_out_bf = jnp.concatenate([w_out[0][perm, :], w_out[0][ATT_WIDTH:, :]], axis=0).astype(BF16)
    aow = attn_out_w[0][perm].reshape(1, ATT_WIDTH)
    x1 = _token_mixer(
        x, mod, pre_w_mix[0].reshape(1, D), w_in_bf, _rope_tables(T),
        attn_sinks[0].reshape(1, ATT_Q_HEADS), aow, lb_table,
        hg_norm_w[0].reshape(1, HG_HEAD_DIM), w_out_bf, post_w_mix[0].reshape(1, D))

    return _channel_mlp(x1, mod, pre_w_mlp[0].reshape(1, D), w_up[0].astype(BF16),
                        w_down[0].astype(BF16), post_w_mlp[0].reshape(1, D))
```
